```python
import math
import jax, jax.numpy as jnp
from jax import lax
import numpy as np

D_MODEL = 1024
BATCH = 2
SEQ = 16384
DEPTH = 1
DEC_BATCH = 2
DEC_SEQ = 8192
PAST_LEN = 128

EPS = 1e-6
NEG = -1e30
MLA_HEADS = 8
MLA_Q_RANK = 256
MLA_KV_RANK = 128
MLA_NOPE_DIM = 64
MLA_ROPE_DIM = 32
MLA_V_DIM = 64
ROPE_THETA = 10000.0
Q_BLOCK = 128
DIL_HEADS = 8
DIL_HEAD_DIM = 64
DIL_PATTERNS = ((128, 1), (512, 4), (2048, 16))
D_MIX = MLA_HEADS * MLA_V_DIM + DIL_HEADS * DIL_HEAD_DIM
IN_SIZES = (MLA_Q_RANK, MLA_KV_RANK, MLA_ROPE_DIM,
            DIL_HEADS * DIL_HEAD_DIM, DIL_HEADS * DIL_HEAD_DIM, DIL_HEADS * DIL_HEAD_DIM)
IN_COLS = sum(IN_SIZES)
N_GROUPS = 4
EXPERTS_PER_GROUP = 4
N_EXPERTS = N_GROUPS * EXPERTS_PER_GROUP
TOP_K = 2
D_EXPERT = 512
MOE_CHUNK = 1024
N_MOD = 6

kernel_name = "hymba_mla_dilated_hmoe_encoder"


def rmsnorm(x, g):
    xf = x.astype(jnp.float32)
    y = xf * lax.rsqrt(jnp.mean(xf * xf, axis=-1, keepdims=True) + EPS)
    return (y * g.astype(jnp.float32)).astype(x.dtype)


def rope_tables(s):
    inv = ROPE_THETA ** (-jnp.arange(0, MLA_ROPE_DIM, 2, dtype=jnp.float32) / MLA_ROPE_DIM)
    ang = jnp.arange(s, dtype=jnp.float32)[:, None] * inv[None, :]
    return jnp.cos(ang), jnp.sin(ang)


def apply_rope(x, cos, sin):
    xf = x.astype(jnp.float32)
    x1, x2 = jnp.split(xf, 2, axis=-1)
    return jnp.concatenate([x1 * cos - x2 * sin, x1 * sin + x2 * cos], axis=-1).astype(x.dtype)


def mla_attention(c_q, c_kv, k_pe, q_norm_g, kv_norm_g, w_uq, w_ukv):
    b, s, _ = c_q.shape
    q = (rmsnorm(c_q, q_norm_g) @ w_uq).reshape(b, s, MLA_HEADS, MLA_NOPE_DIM + MLA_ROPE_DIM)
    q_nope, q_pe = q[..., :MLA_NOPE_DIM], q[..., MLA_NOPE_DIM:]
    kv = (rmsnorm(c_kv, kv_norm_g) @ w_ukv).reshape(b, s, MLA_HEADS, MLA_NOPE_DIM + MLA_V_DIM)
    k_nope, v = kv[..., :MLA_NOPE_DIM], kv[..., MLA_NOPE_DIM:]
    cos, sin = rope_tables(s)
    q_pe = apply_rope(q_pe, cos[None, :, None, :], sin[None, :, None, :])
    k_pe = apply_rope(k_pe, cos[None], sin[None])
    scale = (MLA_NOPE_DIM + MLA_ROPE_DIM) ** -0.5
    nblk = s // Q_BLOCK
    qn_blocks = q_nope.reshape(b, nblk, Q_BLOCK, MLA_HEADS, MLA_NOPE_DIM).transpose(1, 0, 2, 3, 4)
    qp_blocks = q_pe.reshape(b, nblk, Q_BLOCK, MLA_HEADS, MLA_ROPE_DIM).transpose(1, 0, 2, 3, 4)

    def block(args):
        qn, qp = args
        sc = (jnp.einsum('bqhd,bkhd->bhqk', qn, k_nope)
              + jnp.einsum('bqhr,bkr->bhqk', qp, k_pe)).astype(jnp.float32) * scale
        p = jax.nn.softmax(sc, axis=-1).astype(v.dtype)
        return jnp.einsum('bhqk,bkhd->bqhd', p, v)

    out = lax.map(block, (qn_blocks, qp_blocks))
    return out.transpose(1, 0, 2, 3, 4).reshape(b, s, MLA_HEADS * MLA_V_DIM)


def dilated_pattern(q, k, v, slopes, window, dilation):
    b, s, h, dh = q.shape
    r = (window // 2) // dilation
    unit = dilation * r
    s_pad = -(-s // unit) * unit
    L = s_pad // dilation
    nb = L // r

    def strided(t):
        t = jnp.pad(t, ((0, 0), (0, s_pad - s), (0, 0), (0, 0)))
        return t.reshape(b, L, dilation, h, dh).transpose(0, 2, 1, 3, 4)

    def windows(t):
        tp = jnp.pad(t, ((0, 0), (0, 0), (r, r), (0, 0), (0, 0))).reshape(b, dilation, nb + 2, r, h, dh)
        return jnp.concatenate([tp[:, :, :-2], tp[:, :, 1:-1], tp[:, :, 2:]], axis=3)

    qb = strided(q).reshape(b, dilation, nb, r, h, dh)
    kw = windows(strided(k))
    vw = windows(strided(v))
    valid = (jnp.arange(s_pad) < s).reshape(L, dilation).T
    vp = jnp.pad(valid, ((0, 0), (r, r))).reshape(dilation, nb + 2, r)
    key_valid = jnp.concatenate([vp[:, :-2], vp[:, 1:-1], vp[:, 2:]], axis=2)
    rel = jnp.arange(3 * r)[None, :] - r - jnp.arange(r)[:, None]
    in_win = jnp.abs(rel) <= r
    mask = in_win[None, None] & key_valid[:, :, None, :]
    alibi = -slopes[:, None, None] * (jnp.abs(rel) * dilation).astype(jnp.float32)[None]
    sc = jnp.einsum('bcnqhe,bcnkhe->bcnhqk', qb, kw).astype(jnp.float32) * (dh ** -0.5) + alibi
    sc = jnp.where(mask[None, :, :, None], sc, NEG)
    m = jnp.max(sc, axis=-1, keepdims=True)
    e = jnp.exp(sc - m)
    den = jnp.sum(e, axis=-1, keepdims=True)
    p = (e / den).astype(v.dtype)
    out = jnp.einsum('bcnhqk,bcnkhe->bcnqhe', p, vw)
    lse = (m[..., 0] + jnp.log(den[..., 0])).transpose(0, 1, 2, 4, 3)
    out = out.reshape(b, dilation, L, h, dh).transpose(0, 2, 1, 3, 4).reshape(b, s_pad, h, dh)[:, :s]
    lse = lse.reshape(b, dilation, L, h).transpose(0, 2, 1, 3).reshape(b, s_pad, h)[:, :s]
    return out, lse


def dilated_attention(q, k, v):
    b, s, _ = q.shape
    shp = (b, s, DIL_HEADS, DIL_HEAD_DIM)
    q, k, v = q.reshape(shp), k.reshape(shp), v.reshape(shp)
    slopes = jnp.asarray(2.0 ** (-8.0 * np.arange(1, DIL_HEADS + 1) / DIL_HEADS), dtype=jnp.float32)
    res = [dilated_pattern(q, k, v, slopes, w, d) for (w, d) in DIL_PATTERNS]
    outs = jnp.stack([o for o, _ in res], axis=0)
    lses = jnp.stack([l for _, l in res], axis=0)
    wts = jax.nn.softmax(lses, axis=0).astype(outs.dtype)
    return jnp.einsum('pbsh,pbshd->bshd', wts, outs).reshape(b, s, DIL_HEADS * DIL_HEAD_DIM)


def hier_moe(h, w_router_group, w_router_expert, w_gate, w_up, w_down):
    b, s, d = h.shape
    t = h.reshape(b * s, d)
    T = b * s
    gp = jax.nn.softmax((t @ w_router_group).astype(jnp.float32), axis=-1)
    g_idx = jnp.argmax(gp, axis=-1)
    g_w = jnp.max(gp, axis=-1)
    el = (t @ w_router_expert.reshape(d, N_EXPERTS)).reshape(T, N_GROUPS, EXPERTS_PER_GROUP)
    el = jnp.take_along_axis(el, g_idx[:, None, None], axis=1)[:, 0]
    ep = jax.nn.softmax(el.astype(jnp.float32), axis=-1)
    top_v, top_i = lax.top_k(ep, TOP_K)
    top_v = top_v / jnp.sum(top_v, axis=-1, keepdims=True)
    ids = g_idx[:, None] * EXPERTS_PER_GROUP + top_i
    wts = g_w[:, None] * top_v
    combine = jnp.sum(jax.nn.one_hot(ids, N_EXPERTS, dtype=jnp.float32) * wts[..., None], axis=1).astype(t.dtype)
    chunk = math.gcd(T, MOE_CHUNK)
    n = T // chunk

    def run(args):
        tc, wc = args
        a = jnp.einsum('cd,edf->cef', tc, w_gate)
        u = jnp.einsum('cd,edf->cef', tc, w_up)
        hid = jax.nn.silu(a) * u * wc[..., None]
        return jnp.einsum('cef,efd->cd', hid, w_down)

    out = lax.map(run, (t.reshape(n, chunk, d), combine.reshape(n, chunk, N_EXPERTS)))
    return out.reshape(b, s, d)


def encoder_layer(x, c, ada_w, ada_b, norm_mix_g, w_in, q_norm_g, kv_norm_g, w_uq, w_ukv, w_out,
                  norm_moe_g, w_router_group, w_router_expert, w_gate, w_up, w_down):
    mod = (jax.nn.silu(c) @ ada_w + ada_b)[:, None, :]
    shift_a, scale_a, gate_a, shift_m, scale_m, gate_m = jnp.split(mod, N_MOD, axis=-1)
    h = rmsnorm(x, norm_mix_g) * (1 + scale_a) + shift_a
    proj = h @ w_in
    cuts = [int(v) for v in np.cumsum(IN_SIZES)[:-1]]
    c_q, c_kv, k_pe, q_b, k_b, v_b = jnp.split(proj, cuts, axis=-1)
    y_a = mla_attention(c_q, c_kv, k_pe, q_norm_g, kv_norm_g, w_uq, w_ukv)
    y_b = dilated_attention(q_b, k_b, v_b)
    x = x + gate_a * (jnp.concatenate([y_a, y_b], axis=-1) @ w_out)
    h = rmsnorm(x, norm_moe_g) * (1 + scale_m) + shift_m
    x = x + gate_m * hier_moe(h, w_router_group, w_router_expert, w_gate, w_up, w_down)
    return x


def trunk(x, c, ada_w, ada_b, norm_mix_g, w_in, q_norm_g, kv_norm_g, w_uq, w_ukv, w_out,
          norm_moe_g, w_router_group, w_router_expert, w_gate, w_up, w_down, final_norm_g):
    for l in range(DEPTH):
        x = encoder_layer(x, c, ada_w[l], ada_b[l], norm_mix_g[l], w_in[l], q_norm_g[l], kv_norm_g[l],
                          w_uq[l], w_ukv[l], w_out[l], norm_moe_g[l], w_router_group[l],
                          w_router_expert[l], w_gate[l], w_up[l], w_down[l])
    return rmsnorm(x, final_norm_g)


def setup_inputs(seed: int = 0) -> dict:
    key = jax.random.key(seed)
    ks = jax.random.split(key, 24)
    f32 = jnp.float32

    def nrm(k, shape, scale):
        return jax.random.normal(k, shape, f32) * scale

    def gain(k, shape):
        return 1.0 + 0.02 * jax.random.normal(k, shape, f32)

    L = DEPTH
    return {
        "x_prompt": nrm(ks[0], (BATCH, SEQ, D_MODEL), 1.0),
        "x_sample": nrm(ks[1], (DEC_BATCH, DEC_SEQ, D_MODEL), 1.0),
        "c_prompt": nrm(ks[2], (BATCH, D_MODEL), 1.0),
        "c_sample": nrm(ks[3], (DEC_BATCH, D_MODEL), 1.0),
        "ada_w": nrm(ks[4], (L, D_MODEL, N_MOD * D_MODEL), 0.2 * D_MODEL ** -0.5),
        "ada_b": nrm(ks[5], (L, N_MOD * D_MODEL), 0.02),
        "norm_mix_g": gain(ks[6], (L, D_MODEL)),
        "w_in": nrm(ks[7], (L, D_MODEL, IN_COLS), D_MODEL ** -0.5),
        "q_norm_g": gain(ks[8], (L, MLA_Q_RANK)),
        "kv_norm_g": gain(ks[9], (L, MLA_KV_RANK)),
        "w_uq": nrm(ks[10], (L, MLA_Q_RANK, MLA_HEADS * (MLA_NOPE_DIM + MLA_ROPE_DIM)), MLA_Q_RANK ** -0.5),
        "w_ukv": nrm(ks[11], (L, MLA_KV_RANK, MLA_HEADS * (MLA_NOPE_DIM + MLA_V_DIM)), MLA_KV_RANK ** -0.5),
        "w_out": nrm(ks[12], (L, D_MIX, D_MODEL), D_MIX ** -0.5),
        "norm_moe_g": gain(ks[13], (L, D_MODEL)),
        "w_router_group": nrm(ks[14], (L, D_MODEL, N_GROUPS), D_MODEL ** -0.5),
        "w_router_expert": nrm(ks[15], (L, D_MODEL, N_GROUPS, EXPERTS_PER_GROUP), D_MODEL ** -0.5),
        "w_gate": nrm(ks[16], (L, N_EXPERTS, D_MODEL, D_EXPERT), D_MODEL ** -0.5),
        "w_up": nrm(ks[17], (L, N_EXPERTS, D_MODEL, D_EXPERT), D_MODEL ** -0.5),
        "w_down": nrm(ks[18], (L, N_EXPERTS, D_EXPERT, D_MODEL), D_EXPERT ** -0.5),
        "final_norm_g": gain(ks[19], (D_MODEL,)),
    }


def reference(x_prompt, x_sample, c_prompt, c_sample, ada_w, ada_b, norm_mix_g, w_in, q_norm_g, kv_norm_g,
              w_uq, w_ukv, w_out, norm_moe_g, w_router_group, w_router_expert, w_gate, w_up, w_down,
              final_norm_g):
    y_prompt = trunk(x_prompt, c_prompt, ada_w, ada_b, norm_mix_g, w_in, q_norm_g, kv_norm_g, w_uq, w_ukv,
                     w_out, norm_moe_g, w_router_group, w_router_expert, w_gate, w_up, w_down, final_norm_g)
    y_sample = trunk(x_sample, c_sample, ada_w, ada_b, norm_mix_g, w_in, q_norm_g, kv_norm_g, w_uq, w_ukv,
                     w_out, norm_moe_g, w_router_group, w_router_expert, w_gate, w_up, w_down, final_norm_g)
    return (y_prompt, y_sample)
```

```python
import functools
import math

import numpy as np
import jax
import jax.numpy as jnp
from jax import lax
from jax.experimental import pallas as pl
from jax.experimental.pallas import tpu as pltpu

D_MODEL = 1024
EPS = 1e-6
NEG = -1e30
MLA_HEADS = 8
MLA_Q_RANK = 256
MLA_KV_RANK = 128
MLA_NOPE_DIM = 64
MLA_ROPE_DIM = 32
MLA_V_DIM = 64
ROPE_THETA = 10000.0
DIL_HEADS = 8
DIL_HEAD_DIM = 64
DIL_PATTERNS = ((128, 1), (512, 4), (2048, 16))
DIL_WIDTH = DIL_HEADS * DIL_HEAD_DIM
N_GROUPS = 4
EXPERTS_PER_GROUP = 4
N_EXPERTS = N_GROUPS * EXPERTS_PER_GROUP
D_EXPERT = 512
N_MOD = 6

LANES = 128
HEAD_PAIRS = 4
BAND_R = 64
LOG2E = 1.4426950408889634

TM_PROJ = 512
TQ_MLA = 1024
TK_MLA = 1024
TT_DIL = 2048
QB_DIL = 128
TM_OUT = 512
TM_MOE = 1024
VMEM_LIMIT = 56 * 1024 * 1024

F32 = jnp.float32
BF16 = jnp.bfloat16


def _dot(a, b):
    return jnp.dot(a, b, preferred_element_type=F32)


def _dot_nt(a, b):
    return lax.dot_general(a, b, (((1,), (1,)), ((), ())), preferred_element_type=F32)


def _params(sem):
    return pltpu.CompilerParams(dimension_semantics=sem, vmem_limit_bytes=VMEM_LIMIT)


def _mod_kernel(c_ref, w_ref, b_ref, o_ref):
    c = c_ref[...]
    a = c / (1.0 + jnp.exp(-c))
    o_ref[...] = _dot(a.astype(BF16), w_ref[...].astype(BF16)) + b_ref[...]


def _modulation(c_all, ada_w, ada_b):
    rows = c_all.shape[0]
    ncol = ada_w.shape[1]
    tn = 1024
    return pl.pallas_call(
        _mod_kernel,
        out_shape=jax.ShapeDtypeStruct((rows, ncol), F32),
        grid=(ncol // tn,),
        in_specs=[
            pl.BlockSpec((rows, D_MODEL), lambda j: (0, 0)),
            pl.BlockSpec((D_MODEL, tn), lambda j: (0, j)),
            pl.BlockSpec((1, tn), lambda j: (0, j)),
        ],
        out_specs=pl.BlockSpec((rows, tn), lambda j: (0, j)),
        compiler_params=_params(("arbitrary",)),
        name="modulation",
    )(c_all, ada_w, ada_b.reshape(1, ncol))


N_LAT = MLA_Q_RANK + MLA_KV_RANK + 2 * LANES


def _inproj_kernel(x_ref, mod_ref, g_ref, win_ref, qg_ref, kvg_ref, wq_ref, wqr_ref, wk_ref, wv_ref,
                   c1_ref, s1_ref,
                   qm_ref, km_ref, vm_ref,
                   q1_ref, k1_ref, v1_ref, q4_ref, k4_ref, v4_ref, q16_ref, k16_ref, v16_ref,
                   dil_sc, *, tm, q_scale):
    x = x_ref[0]
    shift = mod_ref[0, 0:1, :]
    scale = mod_ref[0, 1:2, :]
    ms = jnp.mean(x * x, axis=-1, keepdims=True)
    h = x * lax.rsqrt(ms + EPS) * g_ref[...]
    h = h * (1.0 + scale) + shift
    hb = h.astype(BF16)

    lat = _dot(hb, win_ref[:, 0:N_LAT])
    dproj = _dot(hb, win_ref[:, N_LAT:])
    for grp in range(3 * HEAD_PAIRS):
        dil_sc[grp] = dproj[:, grp * LANES:(grp + 1) * LANES]

    c1 = c1_ref[...]
    s1 = s1_ref[...]

    c_q = lat[:, 0:MLA_Q_RANK]
    qn = c_q * lax.rsqrt(jnp.mean(c_q * c_q, axis=-1, keepdims=True) + EPS) * qg_ref[...]
    qn = qn.astype(BF16)
    qa = _dot(qn, wq_ref[...])
    qb = _dot(qn, wqr_ref[...])
    for hd in range(MLA_HEADS):
        sl = slice(hd * LANES, (hd + 1) * LANES)
        qm_ref[0, hd] = ((qa[:, sl] * c1 + qb[:, sl] * s1) * q_scale).astype(BF16)

    c_kv = lat[:, MLA_Q_RANK:MLA_Q_RANK + MLA_KV_RANK]
    kvn = c_kv * lax.rsqrt(jnp.mean(c_kv * c_kv, axis=-1, keepdims=True) + EPS) * kvg_ref[...]
    kvn = kvn.astype(BF16)
    off = MLA_Q_RANK + MLA_KV_RANK
    kpe = lat[:, off:off + LANES] * c1 + lat[:, off + LANES:off + 2 * LANES] * s1
    kn = _dot(kvn, wk_ref[...])
    for hd in range(MLA_HEADS):
        sl = slice(hd * LANES, (hd + 1) * LANES)
        km_ref[0, hd] = (kn[:, sl] + kpe).astype(BF16)
    vm_ref[0] = _dot(kvn, wv_ref[...]).astype(BF16)

    outs = ((q1_ref, k1_ref, v1_ref), (q4_ref, k4_ref, v4_ref), (q16_ref, k16_ref, v16_ref))
    for (_, dil), refs in zip(DIL_PATTERNS, outs):
        for t, ref in enumerate(refs):
            mult = DIL_HEAD_DIM ** -0.5 if t == 0 else None
            for grp in range(HEAD_PAIRS):
                cols = slice(grp * LANES, (grp + 1) * LANES)
                for c in range(dil):
                    if dil == 1:
                        rows = dil_sc[t * HEAD_PAIRS + grp]
                    else:
                        rows = dil_sc[t * HEAD_PAIRS + grp, pl.ds(c, tm // dil, stride=dil), :]
                    if mult is not None:
                        rows = rows * mult
                    ref[0, c, :, cols] = rows.astype(BF16)


def _inproj(x, mod, g, w):
    bsz, seq, _ = x.shape
    tm = TM_PROJ
    q_scale = (MLA_NOPE_DIM + MLA_ROPE_DIM) ** -0.5 * LOG2E
    const = lambda b, i: (0, 0)
    head_shape = jax.ShapeDtypeStruct((bsz, MLA_HEADS, seq, LANES), BF16)
    dil_shapes = []
    dil_specs = []
    for _, dil in DIL_PATTERNS:
        for _t in range(3):
            dil_shapes.append(jax.ShapeDtypeStruct((bsz, dil, seq // dil, DIL_WIDTH), BF16))
            dil_specs.append(pl.BlockSpec((1, dil, tm // dil, DIL_WIDTH), lambda b, i: (b, 0, i, 0)))
    return pl.pallas_call(
        functools.partial(_inproj_kernel, tm=tm, q_scale=q_scale),
        out_shape=[head_shape, head_shape, jax.ShapeDtypeStruct((bsz, seq, DIL_WIDTH), BF16)] + dil_shapes,
        grid=(bsz, seq // tm),
        in_specs=[
            pl.BlockSpec((1, tm, D_MODEL), lambda b, i: (b, i, 0)),
            pl.BlockSpec((1, N_MOD, D_MODEL), lambda b, i: (b, 0, 0)),
            pl.BlockSpec((1, D_MODEL), const),
            pl.BlockSpec(w["w_in"].shape, const),
            pl.BlockSpec((1, MLA_Q_RANK), const),
            pl.BlockSpec((1, MLA_KV_RANK), const),
            pl.BlockSpec(w["w_q"].shape, const),
            pl.BlockSpec(w["w_qrot"].shape, const),
            pl.BlockSpec(w["w_k"].shape, const),
            pl.BlockSpec(w["w_v"].shape, const),
            pl.BlockSpec((tm, LANES), lambda b, i: (i, 0)),
            pl.BlockSpec((tm, LANES), lambda b, i: (i, 0)),
        ],
        out_specs=[
            pl.BlockSpec((1, MLA_HEADS, tm, LANES), lambda b, i: (b, 0, i, 0)),
            pl.BlockSpec((1, MLA_HEADS, tm, LANES), lambda b, i: (b, 0, i, 0)),
            pl.BlockSpec((1, tm, DIL_WIDTH), lambda b, i: (b, i, 0)),
        ] + dil_specs,
        scratch_shapes=[pltpu.VMEM((3 * HEAD_PAIRS, tm, LANES), F32)],
        compiler_params=_params(("parallel", "parallel")),
        name="inproj",
    )(x, mod, g, w["w_in"], w["q_norm_g"], w["kv_norm_g"], w["w_q"], w["w_qrot"], w["w_k"], w["w_v"],
      w["rope_c"][:seq], w["rope_s"][:seq])


def _mla_kernel(q_ref, k_ref, v_ref, o_ref, m_sc, l_sc, acc_sc):
    j = pl.program_id(3)

    @pl.when(j == 0)
    def _():
        m_sc[...] = jnp.full(m_sc.shape, NEG, F32)
        l_sc[...] = jnp.zeros(l_sc.shape, F32)
        acc_sc[...] = jnp.zeros(acc_sc.shape, F32)

    v = v_ref[0]
    for hh in range(2):
        s = _dot_nt(q_ref[0, hh], k_ref[0, hh])
        m_prev = m_sc[hh]
        m_new = jnp.maximum(m_prev, jnp.max(s, axis=-1, keepdims=True))
        alpha = jnp.exp2(m_prev - m_new)
        p = jnp.exp2(s - m_new[:, 0:1])
        l_sc[hh] = alpha * l_sc[hh] + jnp.sum(p, axis=-1, keepdims=True)
        acc_sc[hh] = alpha * acc_sc[hh] + _dot(p.astype(BF16), v)
        m_sc[hh] = m_new

    @pl.when(j == pl.num_programs(3) - 1)
    def _():
        lane = lax.broadcasted_iota(jnp.int32, acc_sc.shape[1:], 1)
        o = jnp.where(lane < MLA_V_DIM, acc_sc[0] / l_sc[0], acc_sc[1] / l_sc[1])
        o_ref[0] = o.astype(BF16)


def _mla_attention(q, k, v):
    bsz, _, seq, _ = q.shape
    tq, tk = TQ_MLA, TK_MLA
    return pl.pallas_call(
        _mla_kernel,
        out_shape=jax.ShapeDtypeStruct((bsz, seq, MLA_HEADS * MLA_V_DIM), BF16),
        grid=(bsz, HEAD_PAIRS, seq // tq, seq // tk),
        in_specs=[
            pl.BlockSpec((1, 2, tq, LANES), lambda b, p, i, j: (b, p, i, 0)),
            pl.BlockSpec((1, 2, tk, LANES), lambda b, p, i, j: (b, p, j, 0)),
            pl.BlockSpec((1, tk, LANES), lambda b, p, i, j: (b, j, p)),
        ],
        out_specs=pl.BlockSpec((1, tq, LANES), lambda b, p, i, j: (b, i, p)),
        scratch_shapes=[
            pltpu.VMEM((2, tq, LANES), F32),
            pltpu.VMEM((2, tq, LANES), F32),
            pltpu.VMEM((2, tq, LANES), F32),
        ],
        compiler_params=_params(("parallel", "parallel", "parallel", "arbitrary")),
        name="mla_attention",
    )(q, k, v)


def _dil_kernel(q_ref, kc_ref, kp_ref, kn_ref, vc_ref, vp_ref, vn_ref, bias_ref, o_ref, lse_ref,
                kw_sc, vw_sc, *, dil, lq, seq_l):
    i = pl.program_id(1)
    win = QB_DIL + 2 * BAND_R
    for c in range(dil):
        kw_sc[c, 0:BAND_R] = kp_ref[0, c]
        kw_sc[c, BAND_R:BAND_R + lq] = kc_ref[0, c]
        kw_sc[c, BAND_R + lq:] = kn_ref[0, c]
        vw_sc[c, 0:BAND_R] = vp_ref[0, c]
        vw_sc[c, BAND_R:BAND_R + lq] = vc_ref[0, c]
        vw_sc[c, BAND_R + lq:] = vn_ref[0, c]

    units = lq // QB_DIL
    lane = lax.broadcasted_iota(jnp.int32, (QB_DIL, LANES), 1)
    low = lane < DIL_HEAD_DIM
    col = lax.broadcasted_iota(jnp.int32, (1, win), 1)

    def unit(n, carry):
        c = n // units
        u = n % units
        r0 = pl.multiple_of(u * QB_DIL, QB_DIL)
        kpos = i * lq + r0 - BAND_R + col
        valid = (kpos >= 0) & (kpos < seq_l)
        tok0 = r0 * dil + c
        for g in range(HEAD_PAIRS):
            cols = slice(g * LANES, (g + 1) * LANES)
            qg = q_ref[0, c, pl.ds(r0, QB_DIL), cols]
            kg = kw_sc[c, pl.ds(r0, win), cols]
            vg = vw_sc[c, pl.ds(r0, win), cols]
            o_pair = []
            lse_pair = []
            for hh in range(2):
                qm = jnp.where(low if hh == 0 else ~low, qg, jnp.zeros_like(qg))
                s = _dot_nt(qm, kg) + bias_ref[2 * g + hh]
                s = jnp.where(valid, s, NEG)
                m = jnp.max(s, axis=-1, keepdims=True)
                e = jnp.exp(s - m)
                den = jnp.sum(e, axis=-1, keepdims=True)
                o_pair.append(_dot(e.astype(BF16), vg) / den)
                lse_pair.append(m + jnp.log(den))
            og = jnp.where(low, o_pair[0], o_pair[1])
            lg = jnp.where(low, lse_pair[0], lse_pair[1])
            if dil == 1:
                o_ref[0, g, pl.ds(r0, QB_DIL), :] = og
                lse_ref[0, g, pl.ds(r0, QB_DIL), :] = lg
            else:
                o_ref[0, g, pl.ds(tok0, QB_DIL, stride=dil), :] = og
                lse_ref[0, g, pl.ds(tok0, QB_DIL, stride=dil), :] = lg
        return carry

    lax.fori_loop(0, dil * units, unit, 0)


def _dil_bias(dil):
    rel = np.arange(QB_DIL + 2 * BAND_R)[None, :] - BAND_R - np.arange(QB_DIL)[:, None]
    slopes = 2.0 ** (-8.0 * np.arange(1, DIL_HEADS + 1) / DIL_HEADS)
    bias = -slopes[:, None, None] * (np.abs(rel) * dil).astype(np.float64)[None]
    bias = np.where((np.abs(rel) <= BAND_R)[None], bias, NEG)
    return jnp.asarray(bias, dtype=F32)


def _dilated_pattern(q, k, v, dil):
    bsz, _, seq_l, _ = q.shape
    seq = seq_l * dil
    tt = TT_DIL
    lq = tt // dil
    hb = lq // BAND_R
    n_hb = seq_l // BAND_R
    cur = pl.BlockSpec((1, dil, lq, DIL_WIDTH), lambda b, i: (b, 0, i, 0))
    prev = pl.BlockSpec((1, dil, BAND_R, DIL_WIDTH), lambda b, i: (b, 0, jnp.maximum(i * hb - 1, 0), 0))
    nxt = pl.BlockSpec((1, dil, BAND_R, DIL_WIDTH),
                       lambda b, i: (b, 0, jnp.minimum((i + 1) * hb, n_hb - 1), 0))
    bias = _dil_bias(dil)
    out_spec = pl.BlockSpec((1, HEAD_PAIRS, tt, LANES), lambda b, i: (b, 0, i, 0))
    out_shape = jax.ShapeDtypeStruct((bsz, HEAD_PAIRS, seq, LANES), F32)
    return pl.pallas_call(
        functools.partial(_dil_kernel, dil=dil, lq=lq, seq_l=seq_l),
        out_shape=[out_shape, out_shape],
        grid=(bsz, seq // tt),
        in_specs=[cur, cur, prev, nxt, cur, prev, nxt,
                  pl.BlockSpec(bias.shape, lambda b, i: (0, 0, 0))],
        out_specs=[out_spec, out_spec],
        scratch_shapes=[
            pltpu.VMEM((dil, lq + 2 * BAND_R, DIL_WIDTH), BF16),
            pltpu.VMEM((dil, lq + 2 * BAND_R, DIL_WIDTH), BF16),
        ],
        compiler_params=_params(("parallel", "parallel")),
        name=f"dilated_d{dil}",
    )(q, k, k, k, v, v, v, bias)


def _outproj_kernel(x_ref, mod_ref, ya_ref, o1_ref, o4_ref, o16_ref, l1_ref, l4_ref, l16_ref,
                    wout_ref, g_ref, wrh_ref, wrl_ref,
                    x1_ref, h2_ref, comb_ref):
    yb = []
    for g in range(HEAD_PAIRS):
        l1, l4, l16 = l1_ref[0, g], l4_ref[0, g], l16_ref[0, g]
        mx = jnp.maximum(jnp.maximum(l1, l4), l16)
        e1, e4, e16 = jnp.exp(l1 - mx), jnp.exp(l4 - mx), jnp.exp(l16 - mx)
        den = e1 + e4 + e16
        yg = (e1 / den) * o1_ref[0, g] + (e4 / den) * o4_ref[0, g] + (e16 / den) * o16_ref[0, g]
        yb.append(yg.astype(BF16))
    yb = jnp.concatenate(yb, axis=1)
    half = MLA_HEADS * MLA_V_DIM
    mix = _dot(ya_ref[0], wout_ref[0:half, :]) + _dot(yb, wout_ref[half:, :])
    gate_a = mod_ref[0, 2:3, :]
    x1 = x_ref[0] + gate_a * mix
    x1_ref[0] = x1

    shift = mod_ref[0, 3:4, :]
    scale = mod_ref[0, 4:5, :]
    h = x1 * lax.rsqrt(jnp.mean(x1 * x1, axis=-1, keepdims=True) + EPS) * g_ref[...]
    h = h * (1.0 + scale) + shift
    hb = h.astype(BF16)
    h2_ref[0] = hb

    hl = (h - hb.astype(F32)).astype(BF16)
    logits = _dot(hb, wrh_ref[...]) + (_dot(hb, wrl_ref[...]) + _dot(hl, wrh_ref[...]))
    lane = lax.broadcasted_iota(jnp.int32, logits.shape, 1).astype(F32)
    big = float(LANES)
    gl = jnp.where(lane < N_GROUPS, logits, NEG)
    gmax = jnp.max(gl, axis=-1, keepdims=True)
    gsum = jnp.sum(jnp.exp(gl - gmax), axis=-1, keepdims=True)
    g_w = 1.0 / gsum
    g_idx = jnp.min(jnp.where(gl == gmax, lane, big), axis=-1, keepdims=True)
    e_lo = N_GROUPS + EXPERTS_PER_GROUP * g_idx
    emask = (lane >= e_lo) & (lane < e_lo + EXPERTS_PER_GROUP)
    el = jnp.where(emask, logits, NEG)
    emax = jnp.max(el, axis=-1, keepdims=True)
    ee = jnp.exp(el - emax)
    ep = ee / jnp.sum(ee, axis=-1, keepdims=True)
    v1 = jnp.max(ep, axis=-1, keepdims=True)
    i1 = jnp.min(jnp.where(emask & (ep == v1), lane, big), axis=-1, keepdims=True)
    rest = emask & (lane != i1)
    ep2 = jnp.where(rest, ep, -1.0)
    v2 = jnp.max(ep2, axis=-1, keepdims=True)
    i2 = jnp.min(jnp.where(rest & (ep2 == v2), lane, big), axis=-1, keepdims=True)
    tot = v1 + v2
    comb = jnp.where(lane == i1, g_w * (v1 / tot), 0.0) + jnp.where(lane == i2, g_w * (v2 / tot), 0.0)
    comb_ref[0] = comb


def _outproj(x, mod, ya, dil_outs, w):
    bsz, seq, _ = x.shape
    tm = TM_OUT
    const = lambda b, i: (0, 0)
    tok = lambda width: pl.BlockSpec((1, tm, width), lambda b, i: (b, i, 0))
    grp = pl.BlockSpec((1, HEAD_PAIRS, tm, LANES), lambda b, i: (b, 0, i, 0))
    (o1, l1), (o4, l4), (o16, l16) = dil_outs
    return pl.pallas_call(
        _outproj_kernel,
        out_shape=[
            jax.ShapeDtypeStruct((bsz, seq, D_MODEL), F32),
            jax.ShapeDtypeStruct((bsz, seq, D_MODEL), BF16),
            jax.ShapeDtypeStruct((bsz, seq, LANES), F32),
        ],
        grid=(bsz, seq // tm),
        in_specs=[
            tok(D_MODEL),
            pl.BlockSpec((1, N_MOD, D_MODEL), lambda b, i: (b, 0, 0)),
            tok(DIL_WIDTH), grp, grp, grp, grp, grp, grp,
            pl.BlockSpec(w["w_out"].shape, const),
            pl.BlockSpec((1, D_MODEL), const),
            pl.BlockSpec(w["w_router_hi"].shape, const),
            pl.BlockSpec(w["w_router_lo"].shape, const),
        ],
        out_specs=[tok(D_MODEL), tok(D_MODEL), tok(LANES)],
        compiler_params=_params(("parallel", "parallel")),
        name="outproj",
    )(x, mod, ya, o1, o4, o16, l1, l4, l16, w["w_out"], w["norm_moe_g"], w["w_router_hi"], w["w_router_lo"])


def _moe_kernel(x1_ref, mod_ref, h2_ref, comb_ref, wg_ref, wu_ref, wd_ref, gf_ref, y_ref, acc_sc):
    e = pl.program_id(2)

    @pl.when(e == 0)
    def _():
        acc_sc[...] = jnp.zeros(acc_sc.shape, F32)

    hb = h2_ref[0]
    a = _dot(hb, wg_ref[0])
    u = _dot(hb, wu_ref[0])
    comb = comb_ref[0]
    lane = lax.broadcasted_iota(jnp.int32, comb.shape, 1)
    wc = jnp.sum(jnp.where(lane == e + N_GROUPS, comb, 0.0), axis=-1, keepdims=True)
    hid = (a / (1.0 + jnp.exp(-a))) * u * wc
    acc_sc[...] += _dot(hid.astype(BF16), wd_ref[0])

    @pl.when(e == pl.num_programs(2) - 1)
    def _():
        gate_m = mod_ref[0, 5:6, :]
        x2 = x1_ref[0] + gate_m * acc_sc[...]
        y = x2 * lax.rsqrt(jnp.mean(x2 * x2, axis=-1, keepdims=True) + EPS) * gf_ref[...]
        y_ref[0] = y


def _moe(x1, mod, h2, comb, w):
    bsz, seq, _ = x1.shape
    tm = TM_MOE
    tok = lambda width: pl.BlockSpec((1, tm, width), lambda b, i, e: (b, i, 0))
    return pl.pallas_call(
        _moe_kernel,
        out_shape=jax.ShapeDtypeStruct((bsz, seq, D_MODEL), F32),
        grid=(bsz, seq // tm, N_EXPERTS),
        in_specs=[
            tok(D_MODEL),
            pl.BlockSpec((1, N_MOD, D_MODEL), lambda b, i, e: (b, 0, 0)),
            tok(D_MODEL),
            tok(LANES),
            pl.BlockSpec((1, D_MODEL, D_EXPERT), lambda b, i, e: (e, 0, 0)),
            pl.BlockSpec((1, D_MODEL, D_EXPERT), lambda b, i, e: (e, 0, 0)),
            pl.BlockSpec((1, D_EXPERT, D_MODEL), lambda b, i, e: (e, 0, 0)),
            pl.BlockSpec((1, D_MODEL), lambda b, i, e: (0, 0)),
        ],
        out_specs=tok(D_MODEL),
        scratch_shapes=[pltpu.VMEM((tm, D_MODEL), F32)],
        compiler_params=_params(("parallel", "parallel", "arbitrary")),
        name="moe",
    )(x1, mod, h2, comb, w["w_gate"], w["w_up"], w["w_down"], w["final_norm_g"])


def _rot_cols(wpe):
    half = MLA_ROPE_DIM // 2
    return jnp.concatenate([-wpe[:, half:], wpe[:, :half]], axis=1)


def _pe_group(wpe):
    z = jnp.zeros((wpe.shape[0], MLA_NOPE_DIM), wpe.dtype)
    z2 = jnp.zeros((wpe.shape[0], LANES - MLA_NOPE_DIM - MLA_ROPE_DIM), wpe.dtype)
    return jnp.concatenate([z, wpe, z2], axis=1)


def _prep_weights(max_seq, norm_mix_g, w_in, q_norm_g, kv_norm_g, w_uq, w_ukv, w_out, norm_moe_g,
                  w_router_group, w_router_expert, w_gate, w_up, w_down, final_norm_g):
    cq_end = MLA_Q_RANK
    ckv_end = cq_end + MLA_KV_RANK
    pe_end = ckv_end + MLA_ROPE_DIM
    w_pe = w_in[:, ckv_end:pe_end]
    w_in_p = jnp.concatenate(
        [w_in[:, :ckv_end], _pe_group(w_pe), _pe_group(_rot_cols(w_pe)), w_in[:, pe_end:]], axis=1)

    qk = MLA_NOPE_DIM + MLA_ROPE_DIM
    wq_h = w_uq.reshape(MLA_Q_RANK, MLA_HEADS, qk)
    zpad = jnp.zeros((MLA_Q_RANK, MLA_HEADS, LANES - qk), w_uq.dtype)
    w_q = jnp.concatenate([wq_h, zpad], axis=2).reshape(MLA_Q_RANK, MLA_HEADS * LANES)
    pe_h = wq_h[:, :, MLA_NOPE_DIM:]
    half = MLA_ROPE_DIM // 2
    rot_h = jnp.concatenate([-pe_h[:, :, half:], pe_h[:, :, :half]], axis=2)
    znope = jnp.zeros((MLA_Q_RANK, MLA_HEADS, MLA_NOPE_DIM), w_uq.dtype)
    w_qrot = jnp.concatenate([znope, rot_h, zpad], axis=2).reshape(MLA_Q_RANK, MLA_HEADS * LANES)

    wkv_h = w_ukv.reshape(MLA_KV_RANK, MLA_HEADS, MLA_NOPE_DIM + MLA_V_DIM)
    zk = jnp.zeros((MLA_KV_RANK, MLA_HEADS, LANES - MLA_NOPE_DIM), w_ukv.dtype)
    w_k = jnp.concatenate([wkv_h[:, :, :MLA_NOPE_DIM], zk], axis=2).reshape(MLA_KV_RANK, MLA_HEADS * LANES)
    w_v = wkv_h[:, :, MLA_NOPE_DIM:].reshape(MLA_KV_RANK, MLA_HEADS * MLA_V_DIM)

    inv = ROPE_THETA ** (-jnp.arange(0, MLA_ROPE_DIM, 2, dtype=F32) / MLA_ROPE_DIM)
    ang = jnp.arange(max_seq, dtype=F32)[:, None] * inv[None, :]
    cos, sin = jnp.cos(ang), jnp.sin(ang)
    ones = jnp.ones((max_seq, MLA_NOPE_DIM), F32)
    zeros = jnp.zeros((max_seq, MLA_NOPE_DIM), F32)
    ztail = jnp.zeros((max_seq, LANES - qk), F32)
    rope_c = jnp.concatenate([ones, cos, cos, ztail], axis=1)
    rope_s = jnp.concatenate([zeros, sin, sin, ztail], axis=1)

    w_r = jnp.concatenate(
        [w_router_group, w_router_expert.reshape(D_MODEL, N_EXPERTS),
         jnp.zeros((D_MODEL, LANES - N_GROUPS - N_EXPERTS), F32)], axis=1)
    w_r_hi = w_r.astype(BF16)
    w_r_lo = (w_r - w_r_hi.astype(F32)).astype(BF16)

    return {
        "norm_mix_g": norm_mix_g.reshape(1, D_MODEL),
        "w_in": w_in_p.astype(BF16),
        "q_norm_g": q_norm_g.reshape(1, MLA_Q_RANK),
        "kv_norm_g": kv_norm_g.reshape(1, MLA_KV_RANK),
        "w_q": w_q.astype(BF16),
        "w_qrot": w_qrot.astype(BF16),
        "w_k": w_k.astype(BF16),
        "w_v": w_v.astype(BF16),
        "rope_c": rope_c,
        "rope_s": rope_s,
        "w_out": w_out.astype(BF16),
        "norm_moe_g": norm_moe_g.reshape(1, D_MODEL),
        "w_router_hi": w_r_hi,
        "w_router_lo": w_r_lo,
        "w_gate": w_gate.astype(BF16),
        "w_up": w_up.astype(BF16),
        "w_down": w_down.astype(BF16),
        "final_norm_g": final_norm_g.reshape(1, D_MODEL),
    }


def _trunk(x, mod, w):
    outs = _inproj(x, mod, w["norm_mix_g"], w)
    qm, km, vm = outs[0], outs[1], outs[2]
    ya = _mla_attention(qm, km, vm)
    dil_outs = []
    for n, (_, dil) in enumerate(DIL_PATTERNS):
        q, k, v = outs[3 + 3 * n: 6 + 3 * n]
        dil_outs.append(_dilated_pattern(q, k, v, dil))
    x1, h2, comb = _outproj(x, mod, ya, dil_outs, w)
    return _moe(x1, mod, h2, comb, w)


def kernel(x_prompt, x_sample, c_prompt, c_sample, ada_w, ada_b, norm_mix_g, w_in, q_norm_g, kv_norm_g, w_uq, w_ukv, w_out, norm_moe_g, w_router_group, w_router_expert, w_gate, w_up, w_down, final_norm_g):
    assert ada_w.shape[0] == 1, "single-layer trunk"
    nb_p, nb_s = c_prompt.shape[0], c_sample.shape[0]
    c_all = jnp.concatenate([c_prompt, c_sample], axis=0)
    pad = (-c_all.shape[0]) % 8
    c_all = jnp.pad(c_all, ((0, pad), (0, 0)))
    mod = _modulation(c_all, ada_w[0], ada_b[0]).reshape(-1, N_MOD, D_MODEL)
    w = _prep_weights(max(x_prompt.shape[1], x_sample.shape[1]), norm_mix_g[0], w_in[0], q_norm_g[0],
                      kv_norm_g[0], w_uq[0], w_ukv[0], w_out[0], norm_moe_g[0], w_router_group[0],
                      w_router_expert[0], w_gate[0], w_up[0], w_down[0], final_norm_g)
    y_prompt = _trunk(x_prompt, mod[:nb_p], w)
    y_sample = _trunk(x_sample, mod[nb_p:nb_p + nb_s], w)
    return (y_prompt, y_sample)
```
